```python
import math
import jax, jax.numpy as jnp
from jax import lax
import numpy as np

D_MODEL = 1024
BATCH = 4
SEQ = 8192
DEPTH = 2

GRID_W = 64
CTX_LEN = 256
HEAD_DIM = 64
ROPE_BASE = 10000.0
EPS = 1e-6
NEG_INF = -1e30
Q_BLOCK = 128

NA_HEADS = 8
NA_WIN_R = 8
NA_WIN_C = 16
NA_QCB = 16
NA_KCB = 32
NA_WIDTH = NA_HEADS * HEAD_DIM
NA_SCALE = HEAD_DIM ** -0.5

MLA_HEADS = 8
MLA_Q_LORA = 384
MLA_KV_LORA = 256
MLA_NOPE = 64
MLA_ROPE = 32
MLA_V = 64
MLA_QK = MLA_NOPE + MLA_ROPE
MLA_WIDTH = MLA_HEADS * MLA_V
MLA_SCALE = MLA_QK ** -0.5

DIFF_HEADS = 4
DIFF_D = 64
DIFF_V = 2 * DIFF_D
DIFF_QK_WIDTH = DIFF_HEADS * 2 * DIFF_D
DIFF_WIDTH = DIFF_HEADS * DIFF_V
DIFF_SCALE = DIFF_D ** -0.5

N_BRANCH = 3
BRANCH_WIDTH = 512

IN_SPLITS = (NA_WIDTH, NA_WIDTH, NA_WIDTH,
             MLA_Q_LORA, MLA_KV_LORA, MLA_ROPE,
             DIFF_QK_WIDTH, DIFF_QK_WIDTH, DIFF_WIDTH,
             N_BRANCH * BRANCH_WIDTH, N_BRANCH * D_MODEL)
D_IN = sum(IN_SPLITS)

kernel_name = "hybrid_natten_mla_diffattn_prefix_dit"


def rms_norm(x, g):
    xf = x.astype(jnp.float32)
    y = xf * lax.rsqrt(jnp.mean(xf * xf, axis=-1, keepdims=True) + EPS)
    return (y * g.astype(jnp.float32)).astype(x.dtype)


def axial_rope_tables(n_tokens, rot_dim):
    t = jnp.arange(n_tokens, dtype=jnp.int32)
    row = (t // GRID_W).astype(jnp.float32)
    col = (t % GRID_W).astype(jnp.float32)
    n_freq = rot_dim // 4
    inv = ROPE_BASE ** (-jnp.arange(n_freq, dtype=jnp.float32) / n_freq)
    ang = jnp.concatenate([row[:, None] * inv, col[:, None] * inv], axis=-1)
    return jnp.cos(ang), jnp.sin(ang)


def apply_rope(x, cos, sin):
    half = x.shape[-1] // 2
    shape = (cos.shape[0],) + (1,) * (x.ndim - 3) + (half,)
    cs = cos.reshape(shape).astype(x.dtype)
    sn = sin.reshape(shape).astype(x.dtype)
    x1, x2 = x[..., :half], x[..., half:]
    return jnp.concatenate([x1 * cs - x2 * sn, x2 * cs + x1 * sn], axis=-1)


def rope_tail(x, cos, sin, n_rot):
    return jnp.concatenate([x[..., :-n_rot], apply_rope(x[..., -n_rot:], cos, sin)], axis=-1)


def softmax_attend(q, k, v, scale):
    s = jnp.einsum('bqhd,bkhd->bhqk', q, k).astype(jnp.float32) * scale
    p = jax.nn.softmax(s, axis=-1).astype(v.dtype)
    return jnp.einsum('bhqk,bkhv->bqhv', p, v)


def diff_attend(q, k, v, lam, scale):
    s = jnp.einsum('bqhmd,bkhmd->bhmqk', q, k).astype(jnp.float32) * scale
    p = jax.nn.softmax(s, axis=-1)
    a = (p[:, :, 0] - lam * p[:, :, 1]).astype(v.dtype)
    return jnp.einsum('bhqk,bkhv->bqhv', a, v)


def blocked_queries(fn, q):
    B, T = q.shape[0], q.shape[1]
    nb = T // Q_BLOCK
    qb = jnp.moveaxis(q.reshape((B, nb, Q_BLOCK) + q.shape[2:]), 1, 0)
    out = lax.map(fn, qb)
    out = jnp.moveaxis(out, 0, 1)
    return out.reshape((B, T) + out.shape[3:])


def neighbourhood_attend(q, k, v, k_ctx, v_ctx, rpb):
    B, T, H, d = q.shape
    L = k_ctx.shape[1]
    rows = T // GRID_W
    kr = min(NA_WIN_R, rows)
    ncb = GRID_W // NA_QCB
    qcol = np.arange(GRID_W).reshape(ncb, NA_QCB)
    cstart = np.clip(qcol - NA_WIN_C // 2, 0, GRID_W - NA_WIN_C)
    band0 = np.clip(np.arange(ncb) * NA_QCB - NA_WIN_C // 2, 0, GRID_W - NA_KCB)
    kcol = band0[:, None] + np.arange(NA_KCB)
    kcol_f = np.tile(kcol, (1, kr))
    krow = np.repeat(np.arange(kr), NA_KCB)
    col_in = jnp.asarray((kcol_f[:, None, :] >= cstart[:, :, None])
                         & (kcol_f[:, None, :] < cstart[:, :, None] + NA_WIN_C))
    dc_idx = np.clip(kcol_f[:, None, :] - qcol[:, :, None], -(NA_WIN_C - 1), NA_WIN_C - 1) + (NA_WIN_C - 1)
    nk = kr * NA_KCB
    qg = q.reshape(B, rows, ncb, NA_QCB, H, d)
    kg = k.reshape(B, rows, GRID_W, H, d)
    vg = v.reshape(B, rows, GRID_W, H, d)

    def row_fn(r):
        rs = jnp.clip(r - kr // 2, 0, rows - kr)
        kb = lax.dynamic_slice_in_dim(kg, rs, kr, axis=1)[:, :, kcol]
        vb = lax.dynamic_slice_in_dim(vg, rs, kr, axis=1)[:, :, kcol]
        kb = kb.transpose(0, 2, 1, 3, 4, 5).reshape(B, ncb, nk, H, d)
        vb = vb.transpose(0, 2, 1, 3, 4, 5).reshape(B, ncb, nk, H, d)
        qr = lax.dynamic_index_in_dim(qg, r, axis=1, keepdims=False)
        dr = rs + krow - r + (NA_WIN_R - 1)
        bias = rpb[:, dr[None, None, :], dc_idx].astype(jnp.float32)
        s_lat = jnp.einsum('bnqhd,bnkhd->bhnqk', qr, kb).astype(jnp.float32) * NA_SCALE + bias
        s_lat = jnp.where(col_in, s_lat, NEG_INF)
        s_ctx = jnp.einsum('bnqhd,bkhd->bhnqk', qr, k_ctx).astype(jnp.float32) * NA_SCALE
        p = jax.nn.softmax(jnp.concatenate([s_ctx, s_lat], axis=-1), axis=-1).astype(v.dtype)
        return (jnp.einsum('bhnqk,bkhd->bnqhd', p[..., :L], v_ctx)
                + jnp.einsum('bhnqk,bnkhd->bnqhd', p[..., L:], vb))

    out = lax.map(row_fn, jnp.arange(rows))
    return out.transpose(1, 0, 2, 3, 4, 5).reshape(B, T, H * d)


def mixer_inputs(h, w_in_l, na_q_g, na_k_g, mla_cq_g, mla_ckv_g, w_uq_l, w_ukv_l,
                 mla_q_g, mla_k_g, diff_q_g, diff_k_g):
    lead = h.shape[:-1]
    split_points = [int(s) for s in np.cumsum(IN_SPLITS)[:-1]]
    (na_q, na_k, na_v, cq, ckv, k_rope, dq, dk, dv, z, gm) = jnp.split(h @ w_in_l, split_points, axis=-1)
    na_q = rms_norm(na_q.reshape(lead + (NA_HEADS, HEAD_DIM)), na_q_g)
    na_k = rms_norm(na_k.reshape(lead + (NA_HEADS, HEAD_DIM)), na_k_g)
    na_v = na_v.reshape(lead + (NA_HEADS, HEAD_DIM))
    mq = (rms_norm(cq, mla_cq_g) @ w_uq_l).reshape(lead + (MLA_HEADS, MLA_QK))
    kv = (rms_norm(ckv, mla_ckv_g) @ w_ukv_l).reshape(lead + (MLA_HEADS, MLA_NOPE + MLA_V))
    k_nope, mv = kv[..., :MLA_NOPE], kv[..., MLA_NOPE:]
    k_r = jnp.broadcast_to(k_rope[..., None, :], lead + (MLA_HEADS, MLA_ROPE))
    mk = rms_norm(jnp.concatenate([k_nope, k_r], axis=-1), mla_k_g)
    mq = rms_norm(mq, mla_q_g)
    dq = rms_norm(dq.reshape(lead + (DIFF_HEADS, 2, DIFF_D)), diff_q_g)
    dk = rms_norm(dk.reshape(lead + (DIFF_HEADS, 2, DIFF_D)), diff_k_g)
    dv = dv.reshape(lead + (DIFF_HEADS, DIFF_V))
    return (na_q, na_k, na_v, mq, mk, mv, dq, dk, dv, z, gm)


def diff_finish(o, subln_g, lam_init):
    o = rms_norm(o, subln_g) * (1.0 - lam_init)
    return o.reshape(o.shape[:-2] + (DIFF_WIDTH,))


def merge_branches(o_na, o_mla, o_diff, z, gm, w_br_l, w_out_l):
    z_na, z_mla, z_diff = jnp.split(z, N_BRANCH, axis=-1)
    g_na, g_mla, g_diff = jnp.split(gm, N_BRANCH, axis=-1)
    y = (jax.nn.sigmoid(g_na) * ((o_na * jax.nn.silu(z_na)) @ w_br_l[0])
         + jax.nn.sigmoid(g_mla) * ((o_mla * jax.nn.silu(z_mla)) @ w_br_l[1])
         + jax.nn.sigmoid(g_diff) * ((o_diff * jax.nn.silu(z_diff)) @ w_br_l[2]))
    return y @ w_out_l


def setup_inputs(seed: int = 0) -> dict:
    key = jax.random.key(seed)
    ks = jax.random.split(key, 26)
    f32 = jnp.float32

    def nrm(k, shape, s):
        return jax.random.normal(k, shape, f32) * s

    def gain(k, shape):
        return 1.0 + 0.05 * jax.random.normal(k, shape, f32)

    D = D_MODEL
    return {
        'x': nrm(ks[0], (BATCH, SEQ, D), 1.0),
        'c': nrm(ks[1], (BATCH, D), 1.0),
        'ctx': nrm(ks[2], (BATCH, CTX_LEN, D), 1.0),
        'c_ctx': nrm(ks[3], (D,), 1.0),
        'norm_g': gain(ks[4], (DEPTH, D)),
        'w_ada': nrm(ks[5], (DEPTH, D, 3 * D), D ** -0.5),
        'b_ada': nrm(ks[6], (DEPTH, 3 * D), 0.02),
        'w_in': nrm(ks[7], (DEPTH, D, D_IN), D ** -0.5),
        'na_rpb': nrm(ks[8], (DEPTH, NA_HEADS, 2 * NA_WIN_R - 1, 2 * NA_WIN_C - 1), 0.1),
        'na_q_g': gain(ks[9], (DEPTH, HEAD_DIM)),
        'na_k_g': gain(ks[10], (DEPTH, HEAD_DIM)),
        'mla_cq_g': gain(ks[11], (DEPTH, MLA_Q_LORA)),
        'mla_ckv_g': gain(ks[12], (DEPTH, MLA_KV_LORA)),
        'w_uq': nrm(ks[13], (DEPTH, MLA_Q_LORA, MLA_HEADS * MLA_QK), MLA_Q_LORA ** -0.5),
        'w_ukv': nrm(ks[14], (DEPTH, MLA_KV_LORA, MLA_HEADS * (MLA_NOPE + MLA_V)), MLA_KV_LORA ** -0.5),
        'mla_q_g': gain(ks[15], (DEPTH, MLA_QK)),
        'mla_k_g': gain(ks[16], (DEPTH, MLA_QK)),
        'diff_q_g': gain(ks[17], (DEPTH, DIFF_D)),
        'diff_k_g': gain(ks[18], (DEPTH, DIFF_D)),
        'diff_lq1': nrm(ks[19], (DEPTH, DIFF_D), 0.1),
        'diff_lk1': nrm(ks[20], (DEPTH, DIFF_D), 0.1),
        'diff_lq2': nrm(ks[21], (DEPTH, DIFF_D), 0.1),
        'diff_lk2': nrm(ks[22], (DEPTH, DIFF_D), 0.1),
        'diff_subln_g': gain(ks[23], (DEPTH, DIFF_V)),
        'w_br': nrm(ks[24], (DEPTH, N_BRANCH, BRANCH_WIDTH, D), BRANCH_WIDTH ** -0.5),
        'w_out': nrm(ks[25], (DEPTH, D, D), D ** -0.5),
    }


def reference(x, c, ctx, c_ctx, norm_g, w_ada, b_ada, w_in, na_rpb, na_q_g, na_k_g,
              mla_cq_g, mla_ckv_g, w_uq, w_ukv, mla_q_g, mla_k_g, diff_q_g, diff_k_g,
              diff_lq1, diff_lk1, diff_lq2, diff_lk2, diff_subln_g, w_br, w_out):
    B, T, _ = x.shape
    f32 = jnp.float32
    cos_m, sin_m = axial_rope_tables(T, MLA_ROPE)
    cos_d, sin_d = axial_rope_tables(T, DIFF_D)
    for l in range(DEPTH):
        last = l == DEPTH - 1
        lam_init = 0.8 - 0.6 * math.exp(-0.3 * l)
        lam = (jnp.exp(jnp.sum(diff_lq1[l].astype(f32) * diff_lk1[l].astype(f32)))
               - jnp.exp(jnp.sum(diff_lq2[l].astype(f32) * diff_lk2[l].astype(f32))) + lam_init)
        shift, scale, gate = jnp.split(jax.nn.silu(c) @ w_ada[l] + b_ada[l], 3, axis=-1)
        shift_c, scale_c, gate_c = jnp.split(jax.nn.silu(c_ctx) @ w_ada[l] + b_ada[l], 3, axis=-1)
        h = rms_norm(x, norm_g[l]) * (1.0 + scale[:, None]) + shift[:, None]
        hc = rms_norm(ctx, norm_g[l]) * (1.0 + scale_c) + shift_c
        params = (w_in[l], na_q_g[l], na_k_g[l], mla_cq_g[l], mla_ckv_g[l], w_uq[l], w_ukv[l],
                  mla_q_g[l], mla_k_g[l], diff_q_g[l], diff_k_g[l])
        (na_q, na_k, na_v, m_q, m_k, m_v, d_q, d_k, d_v, z, gm) = mixer_inputs(h, *params)
        (cna_q, cna_k, cna_v, cm_q, cm_k, cm_v, cd_q, cd_k, cd_v, cz, cgm) = mixer_inputs(hc, *params)
        m_q = rope_tail(m_q, cos_m, sin_m, MLA_ROPE)
        m_k = rope_tail(m_k, cos_m, sin_m, MLA_ROPE)
        d_q = apply_rope(d_q, cos_d, sin_d)
        d_k = apply_rope(d_k, cos_d, sin_d)
        m_k_all = jnp.concatenate([cm_k, m_k], axis=1)
        m_v_all = jnp.concatenate([cm_v, m_v], axis=1)
        d_k_all = jnp.concatenate([cd_k, d_k], axis=1)
        d_v_all = jnp.concatenate([cd_v, d_v], axis=1)
        o_na = neighbourhood_attend(na_q, na_k, na_v, cna_k, cna_v, na_rpb[l])
        o_mla = blocked_queries(lambda qb: softmax_attend(qb, m_k_all, m_v_all, MLA_SCALE), m_q)
        o_mla = o_mla.reshape(B, T, MLA_WIDTH)
        o_diff = blocked_queries(lambda qb: diff_attend(qb, d_k_all, d_v_all, lam, DIFF_SCALE), d_q)
        o_diff = diff_finish(o_diff, diff_subln_g[l], lam_init)
        x_new = x + gate[:, None] * merge_branches(o_na, o_mla, o_diff, z, gm, w_br[l], w_out[l])
        if not last:
            co_na = softmax_attend(cna_q, cna_k, cna_v, NA_SCALE).reshape(B, -1, NA_WIDTH)
            co_mla = softmax_attend(cm_q, cm_k, cm_v, MLA_SCALE).reshape(B, -1, MLA_WIDTH)
            co_diff = diff_finish(diff_attend(cd_q, cd_k, cd_v, lam, DIFF_SCALE), diff_subln_g[l], lam_init)
            ctx = ctx + gate_c * merge_branches(co_na, co_mla, co_diff, cz, cgm, w_br[l], w_out[l])
        x = x_new
    return x
```

```python
import functools
import math

import numpy as np
import jax
import jax.numpy as jnp
from jax import lax
from jax.experimental import pallas as pl
from jax.experimental.pallas import tpu as pltpu

F32 = jnp.float32
BF16 = jnp.bfloat16

D_MODEL = 1024
GRID_W = 64
HEAD_DIM = 64
ROPE_BASE = 10000.0
EPS = 1e-6
NEG_INF = -1e30
LOG2E = 1.4426950408889634

NA_HEADS = 8
NA_WIN_R = 8
NA_WIN_C = 16
NA_SCALE = HEAD_DIM ** -0.5
MLA_HEADS = 8
MLA_Q_LORA = 384
MLA_KV_LORA = 256
MLA_NOPE = 64
MLA_ROPE = 32
MLA_QK = MLA_NOPE + MLA_ROPE
MLA_SCALE = MLA_QK ** -0.5
DIFF_HEADS = 4
DIFF_D = 64
DIFF_SCALE = DIFF_D ** -0.5
BRANCH_WIDTH = 512

LANES = 128
VMEM_LIMIT = 56 * 1024 * 1024

_O_NA, _O_CQ, _O_CKV, _O_KR, _O_DIFF, _O_Z, _O_GM, _O_END = 0, 1536, 1920, 2176, 2208, 3744, 5280, 8352
_P_NA, _P_CQ, _P_CKV, _P_KR, _P_DIFF, _P_END = 0, 1536, 1920, 2176, 2304, 3840

_R_NORM, _R_NAQ, _R_NAK, _R_CQ, _R_CKV, _R_MQ, _R_MK, _R_DQ, _R_DK = range(9)

NA_ROWS_PER_STEP = 4
NA_WIN_ROWS = NA_ROWS_PER_STEP + 2 * (NA_WIN_R - 1)
NA_PAD_ROWS = 8

PROJ_TM = 256
MERGE_TM = 256
ATTN_BQ = 256
ATTN_BK = 512


def _cparams(sem):
    return pltpu.CompilerParams(dimension_semantics=sem, vmem_limit_bytes=VMEM_LIMIT)


def _sigmoid(x):
    return 1.0 / (1.0 + jnp.exp(-x))


def _nt_dot(a, b):
    return lax.dot_general(a, b, (((1,), (1,)), ((), ())), preferred_element_type=F32)


def _dot(a, b):
    return jnp.dot(a, b, preferred_element_type=F32)


def _ada_kernel(c_ref, w_ref, b_ref, o_ref):
    c = c_ref[...]
    s = c * _sigmoid(c)
    o_ref[...] = _dot(s.astype(BF16), w_ref[...]) + b_ref[...]


def _ada(cs, w_ada_l, b_ada_l):
    out = pl.pallas_call(
        _ada_kernel,
        out_shape=jax.ShapeDtypeStruct((8, 3 * D_MODEL), F32),
        compiler_params=pltpu.CompilerParams(vmem_limit_bytes=VMEM_LIMIT),
        name="ada",
    )(cs, w_ada_l.astype(BF16), b_ada_l.reshape(1, 3 * D_MODEL))
    return out.reshape(8, 3, D_MODEL)


def _modulated(x, mod_ref, tab_ref):
    g = tab_ref[_R_NORM:_R_NORM + 1, :]
    y = x * lax.rsqrt(jnp.mean(x * x, axis=-1, keepdims=True) + EPS) * g
    return y * (1.0 + mod_ref[0, 1:2, :]) + mod_ref[0, 0:1, :]


def _group_rinv(x, group):
    sq = x * x
    if group == 64:
        lo = lax.broadcasted_iota(jnp.int32, sq.shape, 1) < 64
        s_lo = jnp.sum(jnp.where(lo, sq, 0.0), axis=-1, keepdims=True)
        s_hi = jnp.sum(jnp.where(lo, 0.0, sq), axis=-1, keepdims=True)
        r_lo = lax.rsqrt(s_lo * (1.0 / 64) + EPS)
        r_hi = lax.rsqrt(s_hi * (1.0 / 64) + EPS)
        return jnp.where(lo, r_lo, r_hi)
    s = jnp.sum(sq, axis=-1, keepdims=True)
    return lax.rsqrt(s * (1.0 / group) + EPS)


def _rope_swap(x, half, first):
    return jnp.where(first, pltpu.roll(x, LANES - half, 1), pltpu.roll(x, half, 1))


def _proj_kernel(x_ref, mod_ref, tab_ref, wp_ref, wuq_ref, wukv_ref, cm_ref, sm_ref, cd_ref, sd_ref,
                 naq_ref, nak_ref, nav_ref, mq_ref, mk_ref, mv_ref, dq_ref, dk_ref, dv_ref):
    h = _modulated(x_ref[0], mod_ref, tab_ref).astype(BF16)
    tm = h.shape[0]
    lane = lax.broadcasted_iota(jnp.int32, (tm, LANES), 1)

    na = _dot(h, wp_ref[:, _P_NA:_P_CQ])
    for j in range(4):
        sl = slice(j * LANES, (j + 1) * LANES)
        q = na[:, j * LANES:(j + 1) * LANES]
        naq_ref[0, :, sl] = (q * _group_rinv(q, 64) * tab_ref[_R_NAQ:_R_NAQ + 1, sl]).astype(BF16)
        k = na[:, 512 + j * LANES:512 + (j + 1) * LANES]
        nak_ref[0, :, sl] = (k * _group_rinv(k, 64) * tab_ref[_R_NAK:_R_NAK + 1, sl]).astype(BF16)
    nav_ref[0] = na[:, 1024:1536].astype(BF16)

    lat = _dot(h, wp_ref[:, _P_CQ:_P_DIFF])
    cq = lat[:, 0:MLA_Q_LORA]
    cq = cq * lax.rsqrt(jnp.mean(cq * cq, axis=-1, keepdims=True) + EPS) * tab_ref[_R_CQ:_R_CQ + 1, 0:MLA_Q_LORA]
    ckv = lat[:, MLA_Q_LORA:MLA_Q_LORA + MLA_KV_LORA]
    ckv = ckv * lax.rsqrt(jnp.mean(ckv * ckv, axis=-1, keepdims=True) + EPS) * tab_ref[_R_CKV:_R_CKV + 1, 0:MLA_KV_LORA]
    kr = lat[:, MLA_Q_LORA + MLA_KV_LORA:]
    mq = _dot(cq.astype(BF16), wuq_ref[...])
    kv = _dot(ckv.astype(BF16), wukv_ref[...])
    cm, sm = cm_ref[...], sm_ref[...]
    first_m = (lane >= MLA_NOPE) & (lane < MLA_NOPE + MLA_ROPE // 2)
    for hd in range(MLA_HEADS):
        sl = slice(hd * LANES, (hd + 1) * LANES)
        q = mq[:, hd * LANES:(hd + 1) * LANES]
        q = q * _group_rinv(q, MLA_QK) * tab_ref[_R_MQ:_R_MQ + 1, sl]
        mq_ref[0, :, sl] = (q * cm + _rope_swap(q, MLA_ROPE // 2, first_m) * sm).astype(BF16)
        k = kv[:, hd * LANES:(hd + 1) * LANES] + kr
        k = k * _group_rinv(k, MLA_QK) * tab_ref[_R_MK:_R_MK + 1, sl]
        mk_ref[0, :, sl] = (k * cm + _rope_swap(k, MLA_ROPE // 2, first_m) * sm).astype(BF16)
    mv_ref[0] = kv[:, MLA_HEADS * LANES:].astype(BF16)

    df = _dot(h, wp_ref[:, _P_DIFF:_P_END])
    cd, sd = cd_ref[...], sd_ref[...]
    first_d = (lane % DIFF_D) < DIFF_D // 2
    for j in range(4):
        sl = slice(j * LANES, (j + 1) * LANES)
        q = df[:, j * LANES:(j + 1) * LANES]
        q = q * _group_rinv(q, 64) * tab_ref[_R_DQ:_R_DQ + 1, sl]
        dq_ref[0, :, sl] = (q * cd + _rope_swap(q, DIFF_D // 2, first_d) * sd).astype(BF16)
        k = df[:, 512 + j * LANES:512 + (j + 1) * LANES]
        k = k * _group_rinv(k, 64) * tab_ref[_R_DK:_R_DK + 1, sl]
        dk_ref[0, :, sl] = (k * cd + _rope_swap(k, DIFF_D // 2, first_d) * sd).astype(BF16)
    dv_ref[0] = df[:, 1024:1536].astype(BF16)


def _project(x, mod, mod_row0, per_batch_mod, tabs, wp, wuq, wukv, rope, tm):
    B, T, D = x.shape
    nt = T // tm
    cm, sm, cd, sd = rope
    mod_map = (lambda b, t: (b + mod_row0, 0, 0)) if per_batch_mod else (lambda b, t: (mod_row0, 0, 0))
    const = lambda b, t: (0, 0)
    tok = lambda b, t: (b, t, 0)
    rp = lambda b, t: (t, 0)
    widths = (512, 512, 512, 1024, 1024, 512, 512, 512, 512)
    return pl.pallas_call(
        _proj_kernel,
        grid=(B, nt),
        in_specs=[
            pl.BlockSpec((1, tm, D), tok),
            pl.BlockSpec((1, 3, D), mod_map),
            pl.BlockSpec(tabs.shape, const),
            pl.BlockSpec(wp.shape, const),
            pl.BlockSpec(wuq.shape, const),
            pl.BlockSpec(wukv.shape, const),
            pl.BlockSpec((tm, LANES), rp), pl.BlockSpec((tm, LANES), rp),
            pl.BlockSpec((tm, LANES), rp), pl.BlockSpec((tm, LANES), rp),
        ],
        out_specs=[pl.BlockSpec((1, tm, w), tok) for w in widths],
        out_shape=[jax.ShapeDtypeStruct((B, T, w), BF16) for w in widths],
        compiler_params=_cparams(("parallel", "parallel")),
        name="proj",
    )(x, mod, tabs, wp, wuq, wukv, cm, sm, cd, sd)


def _attn_kernel(*refs, mode, has_latent, n_chunks, bk, lam_init):
    it = iter(refs)
    q_ref, kc_ref, vc_ref = next(it), next(it), next(it)
    k_ref = v_ref = None
    if has_latent:
        k_ref, v_ref = next(it), next(it)
    lamp_ref = sub_ref = None
    if mode == "diff":
        lamp_ref, sub_ref = next(it), next(it)
    o_ref = next(it)
    m_ref, l_ref, acc_ref = next(it), next(it), next(it)

    q = q_ref[0]
    bq = q.shape[0]
    lane = lax.broadcasted_iota(jnp.int32, (bq, LANES), 1)
    lo = lane < 64
    if mode == "pair128":
        qs = (q[:, :LANES], q[:, LANES:])
    else:
        zero = jnp.zeros_like(q)
        qs = (jnp.where(lo, q, zero), jnp.where(lo, zero, q))

    m_ref[...] = jnp.full(m_ref.shape, -jnp.inf, F32)
    l_ref[...] = jnp.zeros(l_ref.shape, F32)
    acc_ref[...] = jnp.zeros(acc_ref.shape, F32)

    def step(k, v):
        for s in range(2):
            ks = k[:, s * LANES:(s + 1) * LANES] if mode == "pair128" else k
            sc = _nt_dot(qs[s], ks)
            m_prev = m_ref[s]
            m_next = jnp.maximum(m_prev, jnp.max(sc, axis=1, keepdims=True))
            alpha = jnp.exp2(m_prev - m_next)
            p = jnp.exp2(sc - m_next[:, 0:1])
            l_ref[s] = alpha * l_ref[s] + jnp.sum(p, axis=1, keepdims=True)
            acc_ref[s] = alpha * acc_ref[s] + _dot(p.astype(BF16), v)
            m_ref[s] = m_next

    step(kc_ref[0], vc_ref[0])
    if has_latent:
        def body(c, carry):
            start = pl.multiple_of(c * bk, bk)
            step(k_ref[0, pl.ds(start, bk), :], v_ref[0, pl.ds(start, bk), :])
            return carry
        lax.fori_loop(0, n_chunks, body, 0)

    oa = acc_ref[0] / l_ref[0]
    ob = acc_ref[1] / l_ref[1]
    if mode == "diff":
        lp = lamp_ref[...]
        e1 = jnp.exp(jnp.sum(lp[0:1] * lp[1:2], axis=-1, keepdims=True))
        e2 = jnp.exp(jnp.sum(lp[2:3] * lp[3:4], axis=-1, keepdims=True))
        lam = e1 - e2 + lam_init
        o = oa - lam * ob
        o = o * lax.rsqrt(jnp.mean(o * o, axis=-1, keepdims=True) + EPS) * sub_ref[...]
        o_ref[0] = (o * (1.0 - lam_init)).astype(o_ref.dtype)
    else:
        o_ref[0] = jnp.where(lo, oa, ob).astype(o_ref.dtype)


def _attend(q, kc, vc, k, v, mode, lam_init=0.0, lamp=None, subln=None, bq=ATTN_BQ, bk=ATTN_BK):
    B, Tq, _ = q.shape
    L = kc.shape[1]
    wq = 2 * LANES if mode == "pair128" else LANES
    G = q.shape[2] // wq
    bq = min(bq, Tq)
    nq = Tq // bq
    has_latent = k is not None
    qmap = lambda b, g, i: (b, i, g)
    kmap = lambda b, g, i: (b, 0, g)
    in_specs = [pl.BlockSpec((1, bq, wq), qmap),
                pl.BlockSpec((1, L, wq), kmap),
                pl.BlockSpec((1, L, LANES), kmap)]
    args = [q, kc, vc]
    n_chunks = 0
    if has_latent:
        T = k.shape[1]
        n_chunks = T // bk
        in_specs += [pl.BlockSpec((1, T, wq), kmap), pl.BlockSpec((1, T, LANES), kmap)]
        args += [k, v]
    if mode == "diff":
        in_specs += [pl.BlockSpec(lamp.shape, lambda b, g, i: (0, 0)),
                     pl.BlockSpec(subln.shape, lambda b, g, i: (0, 0))]
        args += [lamp, subln]
    kern = functools.partial(_attn_kernel, mode=mode, has_latent=has_latent, n_chunks=n_chunks, bk=bk,
                             lam_init=lam_init)
    return pl.pallas_call(
        kern,
        grid=(B, G, nq),
        in_specs=in_specs,
        out_specs=pl.BlockSpec((1, bq, LANES), qmap),
        out_shape=jax.ShapeDtypeStruct((B, Tq, G * LANES), BF16),
        scratch_shapes=[pltpu.VMEM((2, bq, LANES), F32), pltpu.VMEM((2, bq, LANES), F32),
                        pltpu.VMEM((2, bq, LANES), F32)],
        compiler_params=_cparams(("parallel", "parallel", "arbitrary")),
        name="attn_" + mode,
    )(*args)


def _na_kernel(q_ref, kc_ref, vc_ref, k_ref, v_ref, bias_ref, o_ref, *, rows):
    rg = pl.program_id(2)
    r0 = rg * NA_ROWS_PER_STEP
    nq = NA_ROWS_PER_STEP * GRID_W
    nk = NA_WIN_ROWS * GRID_W
    start = pl.multiple_of((r0 + NA_PAD_ROWS - (NA_WIN_R - 1)) * GRID_W, GRID_W)
    kw = k_ref[0, pl.ds(start, nk), :]
    vw = v_ref[0, pl.ds(start, nk), :]
    kc = kc_ref[0]
    vc = vc_ref[0]
    q = q_ref[0]

    qrow = lax.broadcasted_iota(jnp.int32, (nq, 1), 0) // GRID_W
    lo_idx = jnp.zeros((nq, 1), jnp.int32)
    for j in range(NA_ROWS_PER_STEP):
        rs = jnp.clip(r0 + j - NA_WIN_R // 2, 0, rows - NA_WIN_R)
        lo_j = (rs - r0 + (NA_WIN_R - 1)) * GRID_W
        lo_idx = jnp.where(qrow == j, lo_j, lo_idx)
    kidx = lax.broadcasted_iota(jnp.int32, (nq, nk), 1)
    valid = (kidx >= lo_idx) & (kidx < lo_idx + NA_WIN_R * GRID_W)

    lane = lax.broadcasted_iota(jnp.int32, (nq, LANES), 1)
    lo = lane < 64
    zero = jnp.zeros_like(q)
    outs = []
    for s in range(2):
        qs = jnp.where(lo, q, zero) if s == 0 else jnp.where(lo, zero, q)
        s_c = _nt_dot(qs, kc)
        s_l = jnp.where(valid, _nt_dot(qs, kw) + bias_ref[s], NEG_INF)
        m = jnp.maximum(jnp.max(s_c, axis=1, keepdims=True), jnp.max(s_l, axis=1, keepdims=True))
        p_c = jnp.exp2(s_c - m)
        p_l = jnp.exp2(s_l - m)
        l = jnp.sum(p_c, axis=1, keepdims=True) + jnp.sum(p_l, axis=1, keepdims=True)
        o = _dot(p_c.astype(BF16), vc) + _dot(p_l.astype(BF16), vw)
        outs.append(o / l)
    o_ref[0] = jnp.where(lo, outs[0], outs[1]).astype(o_ref.dtype)


def _na_attend(q, kc, vc, kpad, vpad, bias):
    B, T, _ = q.shape
    rows = T // GRID_W
    L = kc.shape[1]
    Tp = kpad.shape[1]
    nq = NA_ROWS_PER_STEP * GRID_W
    nk = NA_WIN_ROWS * GRID_W
    qmap = lambda b, g, i: (b, i, g)
    kmap = lambda b, g, i: (b, 0, g)
    return pl.pallas_call(
        functools.partial(_na_kernel, rows=rows),
        grid=(B, NA_HEADS // 2, rows // NA_ROWS_PER_STEP),
        in_specs=[pl.BlockSpec((1, nq, LANES), qmap),
                  pl.BlockSpec((1, L, LANES), kmap),
                  pl.BlockSpec((1, L, LANES), kmap),
                  pl.BlockSpec((1, Tp, LANES), kmap),
                  pl.BlockSpec((1, Tp, LANES), kmap),
                  pl.BlockSpec((2, nq, nk), lambda b, g, i: (g, 0, 0))],
        out_specs=pl.BlockSpec((1, nq, LANES), qmap),
        out_shape=jax.ShapeDtypeStruct((B, T, NA_HEADS * HEAD_DIM), BF16),
        compiler_params=_cparams(("parallel", "parallel", "arbitrary")),
        name="na",
    )(q, kc, vc, kpad, vpad, bias)


def _na_bias_index():
    j = np.arange(NA_ROWS_PER_STEP)[:, None, None, None]
    qc = np.arange(GRID_W)[None, :, None, None]
    i = np.arange(NA_WIN_ROWS)[None, None, :, None]
    kc = np.arange(GRID_W)[None, None, None, :]
    dr = i - (NA_WIN_R - 1) - j
    cstart = np.clip(qc - NA_WIN_C // 2, 0, GRID_W - NA_WIN_C)
    ok = (np.abs(dr) <= NA_WIN_R - 1) & (kc >= cstart) & (kc < cstart + NA_WIN_C)
    dc = np.clip(kc - qc, -(NA_WIN_C - 1), NA_WIN_C - 1) + (NA_WIN_C - 1)
    flat = (np.clip(dr, -(NA_WIN_R - 1), NA_WIN_R - 1) + (NA_WIN_R - 1)) * (2 * NA_WIN_C - 1) + dc
    shape = (NA_ROWS_PER_STEP * GRID_W, NA_WIN_ROWS * GRID_W)
    return (np.broadcast_to(flat, ok.shape).reshape(shape).astype(np.int32),
            np.broadcast_to(ok, ok.shape).reshape(shape))


def _merge_kernel(x_ref, mod_ref, tab_ref, wzg_ref, ona_ref, omla_ref, odiff_ref, wbr_ref, wout_ref, o_ref):
    x = x_ref[0]
    h = _modulated(x, mod_ref, tab_ref).astype(BF16)
    y = None
    for i, oref in enumerate((ona_ref, omla_ref, odiff_ref)):
        z = _dot(h, wzg_ref[:, i * BRANCH_WIDTH:(i + 1) * BRANCH_WIDTH])
        g = _dot(h, wzg_ref[:, 3 * BRANCH_WIDTH + i * D_MODEL:3 * BRANCH_WIDTH + (i + 1) * D_MODEL])
        u = oref[0].astype(F32) * (z * _sigmoid(z))
        t = _sigmoid(g) * _dot(u.astype(BF16), wbr_ref[i])
        y = t if y is None else y + t
    out = _dot(y.astype(BF16), wout_ref[...])
    o_ref[0] = x + mod_ref[0, 2:3, :] * out


def _merge(x, mod, mod_row0, per_batch_mod, tabs, wzg, o_na, o_mla, o_diff, wbr, wout, tm):
    B, T, D = x.shape
    nt = T // tm
    mod_map = (lambda b, t: (b + mod_row0, 0, 0)) if per_batch_mod else (lambda b, t: (mod_row0, 0, 0))
    tok = lambda b, t: (b, t, 0)
    return pl.pallas_call(
        _merge_kernel,
        grid=(B, nt),
        in_specs=[pl.BlockSpec((1, tm, D), tok),
                  pl.BlockSpec((1, 3, D), mod_map),
                  pl.BlockSpec(tabs.shape, lambda b, t: (0, 0)),
                  pl.BlockSpec(wzg.shape, lambda b, t: (0, 0)),
                  pl.BlockSpec((1, tm, BRANCH_WIDTH), tok),
                  pl.BlockSpec((1, tm, BRANCH_WIDTH), tok),
                  pl.BlockSpec((1, tm, BRANCH_WIDTH), tok),
                  pl.BlockSpec(wbr.shape, lambda b, t: (0, 0, 0)),
                  pl.BlockSpec(wout.shape, lambda b, t: (0, 0))],
        out_specs=pl.BlockSpec((1, tm, D), tok),
        out_shape=jax.ShapeDtypeStruct((B, T, D), F32),
        compiler_params=_cparams(("parallel", "parallel")),
        name="merge",
    )(x, mod, tabs, wzg, o_na, o_mla, o_diff, wbr, wout)


def _rope_tables(T):
    t = np.arange(T)
    row = (t // GRID_W).astype(np.float32)
    col = (t % GRID_W).astype(np.float32)

    def cs(rot):
        nf = rot // 4
        inv = jnp.asarray(ROPE_BASE, F32) ** (-jnp.arange(nf, dtype=F32) / nf)
        ang = jnp.concatenate([jnp.asarray(row)[:, None] * inv, jnp.asarray(col)[:, None] * inv], axis=-1)
        return jnp.cos(ang), jnp.sin(ang)

    c_m, s_m = cs(MLA_ROPE)
    c_d, s_d = cs(DIFF_D)
    ones = jnp.ones((T, MLA_NOPE), F32)
    zeros64 = jnp.zeros((T, MLA_NOPE), F32)
    zeros32 = jnp.zeros((T, LANES - MLA_QK), F32)
    cm = jnp.concatenate([ones, c_m, c_m, zeros32], axis=-1)
    sm = jnp.concatenate([zeros64, -s_m, s_m, zeros32], axis=-1)
    cd = jnp.concatenate([c_d, c_d, c_d, c_d], axis=-1)
    sd = jnp.concatenate([-s_d, s_d, -s_d, s_d], axis=-1)
    return cm, sm, cd, sd


def _identity_rope(L):
    one = jnp.ones((L, LANES), F32)
    zero = jnp.zeros((L, LANES), F32)
    return one, zero, one, zero


def _row(v, scale=1.0):
    v = v.astype(F32) * scale
    return jnp.pad(v, (0, D_MODEL - v.shape[0]))


def _layer_tables(l, norm_g, na_q_g, na_k_g, mla_cq_g, mla_ckv_g, mla_q_g, mla_k_g, diff_q_g, diff_k_g):
    pad_head = lambda g: jnp.tile(jnp.pad(g.astype(F32), (0, LANES - MLA_QK)), MLA_HEADS)
    rows = [
        _row(norm_g[l]),
        _row(jnp.tile(na_q_g[l], NA_HEADS), NA_SCALE * LOG2E),
        _row(jnp.tile(na_k_g[l], NA_HEADS)),
        _row(mla_cq_g[l]),
        _row(mla_ckv_g[l]),
        _row(pad_head(mla_q_g[l]), MLA_SCALE * LOG2E),
        _row(pad_head(mla_k_g[l])),
        _row(jnp.tile(diff_q_g[l], 2 * DIFF_HEADS), DIFF_SCALE * LOG2E),
        _row(jnp.tile(diff_k_g[l], 2 * DIFF_HEADS)),
    ]
    rows += [jnp.zeros((D_MODEL,), F32)] * (16 - len(rows))
    return jnp.stack(rows)


def _layer_weights(l, w_in, w_uq, w_ukv):
    w = w_in[l]
    kr = jnp.pad(w[:, _O_KR:_O_DIFF], ((0, 0), (MLA_NOPE, LANES - MLA_QK)))
    wp = jnp.concatenate([w[:, _O_NA:_O_KR], kr, w[:, _O_DIFF:_O_Z]], axis=1).astype(BF16)
    wzg = w[:, _O_Z:_O_END].astype(BF16)
    wuq = jnp.pad(w_uq[l].reshape(MLA_Q_LORA, MLA_HEADS, MLA_QK), ((0, 0), (0, 0), (0, LANES - MLA_QK)))
    wuq = wuq.reshape(MLA_Q_LORA, MLA_HEADS * LANES).astype(BF16)
    ukv = w_ukv[l].reshape(MLA_KV_LORA, MLA_HEADS, 2 * MLA_NOPE)
    wk = jnp.pad(ukv[:, :, :MLA_NOPE], ((0, 0), (0, 0), (0, LANES - MLA_NOPE))).reshape(MLA_KV_LORA, MLA_HEADS * LANES)
    wv = ukv[:, :, MLA_NOPE:].reshape(MLA_KV_LORA, MLA_HEADS * MLA_NOPE)
    wukv = jnp.concatenate([wk, wv], axis=1).astype(BF16)
    return wp, wzg, wuq, wukv


def kernel(x, c, ctx, c_ctx, norm_g, w_ada, b_ada, w_in, na_rpb, na_q_g, na_k_g, mla_cq_g, mla_ckv_g, w_uq, w_ukv,
           mla_q_g, mla_k_g, diff_q_g, diff_k_g, diff_lq1, diff_lk1, diff_lq2, diff_lk2, diff_subln_g, w_br, w_out):
    B, T, D = x.shape
    L = ctx.shape[1]
    depth = w_in.shape[0]
    rope_lat = _rope_tables(T)
    rope_ctx = _identity_rope(L)
    bias_idx, bias_ok = _na_bias_index()
    cs = jnp.zeros((8, D), F32).at[:B].set(c).at[B].set(c_ctx)
    pad_tok = NA_PAD_ROWS * GRID_W

    for l in range(depth):
        last = l == depth - 1
        lam_init = 0.8 - 0.6 * math.exp(-0.3 * l)
        mod = _ada(cs, w_ada[l], b_ada[l])
        tabs = _layer_tables(l, norm_g, na_q_g, na_k_g, mla_cq_g, mla_ckv_g, mla_q_g, mla_k_g, diff_q_g, diff_k_g)
        wp, wzg, wuq, wukv = _layer_weights(l, w_in, w_uq, w_ukv)
        wbr = w_br[l].astype(BF16)
        wout = w_out[l].astype(BF16)
        lamp = jnp.pad(jnp.stack([diff_lq1[l], diff_lk1[l], diff_lq2[l], diff_lk2[l]]).astype(F32),
                       ((0, 0), (0, LANES - DIFF_D)))
        subln = diff_subln_g[l].astype(F32).reshape(1, LANES)
        rpb_flat = na_rpb[l].reshape(NA_HEADS, -1).astype(F32) * LOG2E
        bias = jnp.where(bias_ok[None], jnp.take(rpb_flat, bias_idx, axis=1), NEG_INF)

        naq, nak, nav, mq, mk, mv, dq, dk, dv = _project(x, mod, 0, True, tabs, wp, wuq, wukv, rope_lat, PROJ_TM)
        cnaq, cnak, cnav, cmq, cmk, cmv, cdq, cdk, cdv = _project(ctx, mod, B, False, tabs, wp, wuq, wukv,
                                                                 rope_ctx, min(PROJ_TM, L))
        kpad = jnp.pad(nak, ((0, 0), (pad_tok, pad_tok), (0, 0)))
        vpad = jnp.pad(nav, ((0, 0), (pad_tok, pad_tok), (0, 0)))
        o_na = _na_attend(naq, cnak, cnav, kpad, vpad, bias)
        o_mla = _attend(mq, cmk, cmv, mk, mv, "pair128")
        o_diff = _attend(dq, cdk, cdv, dk, dv, "diff", lam_init, lamp, subln)
        x_new = _merge(x, mod, 0, True, tabs, wzg, o_na, o_mla, o_diff, wbr, wout, MERGE_TM)
        if not last:
            co_na = _attend(cnaq, cnak, cnav, None, None, "pair64")
            co_mla = _attend(cmq, cmk, cmv, None, None, "pair128")
            co_diff = _attend(cdq, cdk, cdv, None, None, "diff", lam_init, lamp, subln)
            ctx = _merge(ctx, mod, B, False, tabs, wzg, co_na, co_mla, co_diff, wbr, wout, min(MERGE_TM, L))
        x = x_new
    return x
```
